```python
import jax, jax.numpy as jnp
from jax import lax
import numpy as np

D_MODEL = 1024
BATCH = 8
SEQ = 4096
DEPTH = 2
DEC_BATCH = 8
DEC_SEQ = 16
PAST_LEN = 1024

CHUNK = 64
N_EVEN = (DEPTH + 1) // 2
N_ODD = DEPTH // 2
MIX_W = D_MODEL // 2
PROJ_W = 5 * MIX_W
CONV_A_W = 31
CONV_B_W = 3
HEAD_DIM_C = 64
N_HEADS_C = MIX_W // HEAD_DIM_C
BAND_CHUNKS = 8
ATT_WINDOW = BAND_CHUNKS * CHUNK
BAND = ATT_WINDOW + CHUNK
REL_CLIP = 4 * CHUNK
SGU_CHUNK = 128
SGU_GROUPS = 4
SGU_GDIM = MIX_W // SGU_GROUPS
FFN_HIDDEN = ((8 * D_MODEL // 3 + 255) // 256) * 256
NEG_INF = -1e30

kernel_name = 'hybrid_streaming_conv_band_attn_sgu_step'


def rmsnorm(x, g, eps=1e-6):
    xf = x.astype(jnp.float32)
    y = xf * lax.rsqrt(jnp.mean(xf * xf, axis=-1, keepdims=True) + eps)
    return (y * g.astype(jnp.float32)).astype(x.dtype)


def layernorm(x, g, b, eps=1e-5):
    xf = x.astype(jnp.float32)
    mu = jnp.mean(xf, axis=-1, keepdims=True)
    xc = xf - mu
    var = jnp.mean(xc * xc, axis=-1, keepdims=True)
    y = xc * lax.rsqrt(var + eps) * g.astype(jnp.float32) + b.astype(jnp.float32)
    return y.astype(x.dtype)


def causal_dwconv(x, prev, w):
    xp = jnp.concatenate([prev.astype(x.dtype), x], axis=1)
    y = lax.conv_general_dilated(xp, w[:, None, :].astype(x.dtype), window_strides=(1,),
                                 padding='VALID', dimension_numbers=('NWC', 'WIO', 'NWC'),
                                 feature_group_count=x.shape[-1])
    return y, xp[:, -(w.shape[0] - 1):]


def conv_mixers(h, prev_a, prev_b, w_in, a_w, a_b, a_ln_g, a_ln_b, b_w, w_out):
    p = h @ w_in
    a_val, a_gate, g_b, g_c, b_h = jnp.split(p, 5, axis=-1)
    a = a_val * jax.nn.sigmoid(a_gate)
    a_conv, new_a = causal_dwconv(a, prev_a, a_w)
    a_out = jax.nn.silu(layernorm(a_conv + a_b.astype(a_conv.dtype), a_ln_g, a_ln_b))
    z = g_c * b_h
    z_conv, new_b = causal_dwconv(z, prev_b, b_w)
    b_out = g_b * z_conv
    return jnp.concatenate([a_out, b_out], axis=-1) @ w_out, new_a, new_b


def odd_project(h, w_in, q_g, k_g):
    p = h @ w_in
    q, k, v, u, sv = jnp.split(p, 5, axis=-1)
    B, T = h.shape[:2]
    q = rmsnorm(q.reshape(B, T, N_HEADS_C, HEAD_DIM_C), q_g)
    k = rmsnorm(k.reshape(B, T, N_HEADS_C, HEAD_DIM_C), k_g)
    v = v.reshape(B, T, N_HEADS_C, HEAD_DIM_C)
    return q, k, v, u, sv


def rel_bias_lookup(rel_bias, d):
    return rel_bias[:, jnp.clip(d, -REL_CLIP, REL_CLIP) + REL_CLIP].astype(jnp.float32)


def band_attention_prompt(q, k, v, rel_bias):
    B, T = q.shape[:2]
    nc = T // CHUNK
    pad = jnp.zeros((B, ATT_WINDOW, N_HEADS_C, HEAD_DIM_C), k.dtype)
    kp = jnp.concatenate([pad, k], axis=1)
    vp = jnp.concatenate([pad, v], axis=1)
    idx = jnp.arange(nc)[:, None] * CHUNK + jnp.arange(BAND)[None, :]
    kb = kp[:, idx]
    vb = vp[:, idx]
    qc = q.reshape(B, nc, CHUNK, N_HEADS_C, HEAD_DIM_C)
    s = jnp.einsum('bcqhd,bckhd->bhcqk', qc, kb).astype(jnp.float32) * (HEAD_DIM_C ** -0.5)
    d = jnp.arange(CHUNK)[:, None] + ATT_WINDOW - jnp.arange(BAND)[None, :]
    s = s + rel_bias_lookup(rel_bias, d)[None, :, None]
    valid = idx >= ATT_WINDOW
    s = jnp.where(valid[None, None, :, None, :], s, NEG_INF)
    pr = jax.nn.softmax(s, axis=-1).astype(v.dtype)
    o = jnp.einsum('bhcqk,bckhd->bcqhd', pr, vb)
    return o.reshape(B, T, MIX_W)


def band_attention_sample(q, k, v, cache_k, cache_v, rel_bias):
    B, Tn = q.shape[:2]
    L = cache_k.shape[1]
    kf = jnp.concatenate([cache_k.astype(k.dtype), k], axis=1)
    vf = jnp.concatenate([cache_v.astype(v.dtype), v], axis=1)
    s = jnp.einsum('bqhd,bkhd->bhqk', q, kf).astype(jnp.float32) * (HEAD_DIM_C ** -0.5)
    d = jnp.arange(Tn)[:, None] - (jnp.arange(L + Tn)[None, :] - L)
    s = s + rel_bias_lookup(rel_bias, d)[None]
    pr = jax.nn.softmax(s, axis=-1).astype(v.dtype)
    o = jnp.einsum('bhqk,bkhd->bqhd', pr, vf)
    return o.reshape(B, Tn, MIX_W)


def spatial_gating(u, sv, ln_g, ln_b, w_s, b_s):
    B, T = u.shape[:2]
    lc = min(T, SGU_CHUNK)
    nc = T // lc
    svn = layernorm(sv, ln_g, ln_b)
    vc = svn.reshape(B, nc, lc, SGU_GROUPS, SGU_GDIM)
    wm = jnp.tril(w_s[:, :lc, :lc]).astype(u.dtype)
    s = jnp.einsum('gij,bcjgd->bcigd', wm, vc) + jnp.transpose(b_s[:, :lc]).astype(u.dtype)[:, :, None]
    return u * s.reshape(B, T, MIX_W), svn


def swiglu(h, w_gate_up, w_down):
    g, up = jnp.split(h @ w_gate_up, 2, axis=-1)
    return (jax.nn.silu(g) * up) @ w_down


def setup_inputs(seed: int = 0) -> dict:
    key = jax.random.key(seed)
    ks = jax.random.split(key, 32)
    att_cache = min(ATT_WINDOW, PAST_LEN)

    def nrm(k, shape, scale):
        return jax.random.normal(k, shape, jnp.float32) * scale

    return {
        'x_prompt': nrm(ks[0], (BATCH, SEQ, D_MODEL), 1.0),
        'x_sample': nrm(ks[1], (DEC_BATCH, DEC_SEQ, D_MODEL), 1.0),
        'cache_conv_a': nrm(ks[2], (N_EVEN, DEC_BATCH, CONV_A_W - 1, MIX_W), 0.5),
        'cache_conv_b': nrm(ks[3], (N_EVEN, DEC_BATCH, CONV_B_W - 1, MIX_W), 0.5),
        'cache_k': nrm(ks[4], (N_ODD, DEC_BATCH, att_cache, N_HEADS_C, HEAD_DIM_C), 1.0),
        'cache_v': nrm(ks[5], (N_ODD, DEC_BATCH, att_cache, N_HEADS_C, HEAD_DIM_C), 1.0),
        'norm_mix_even': 1.0 + nrm(ks[6], (N_EVEN, D_MODEL), 0.1),
        'w_in_even': nrm(ks[7], (N_EVEN, D_MODEL, PROJ_W), D_MODEL ** -0.5),
        'conv_a_w': nrm(ks[8], (N_EVEN, CONV_A_W, MIX_W), CONV_A_W ** -0.5),
        'conv_a_b': nrm(ks[9], (N_EVEN, MIX_W), 0.01),
        'ln_a_g': 1.0 + nrm(ks[10], (N_EVEN, MIX_W), 0.1),
        'ln_a_b': nrm(ks[11], (N_EVEN, MIX_W), 0.01),
        'conv_b_w': nrm(ks[12], (N_EVEN, CONV_B_W, MIX_W), CONV_B_W ** -0.5),
        'w_out_even': nrm(ks[13], (N_EVEN, 2 * MIX_W, D_MODEL), (2 * MIX_W) ** -0.5),
        'norm_mix_odd': 1.0 + nrm(ks[14], (N_ODD, D_MODEL), 0.1),
        'w_in_odd': nrm(ks[15], (N_ODD, D_MODEL, PROJ_W), D_MODEL ** -0.5),
        'q_norm_g': 1.0 + nrm(ks[16], (N_ODD, HEAD_DIM_C), 0.1),
        'k_norm_g': 1.0 + nrm(ks[17], (N_ODD, HEAD_DIM_C), 0.1),
        'rel_bias': nrm(ks[18], (N_ODD, N_HEADS_C, 2 * REL_CLIP + 1), 0.1),
        'sgu_ln_g': 1.0 + nrm(ks[19], (N_ODD, MIX_W), 0.1),
        'sgu_ln_b': nrm(ks[20], (N_ODD, MIX_W), 0.01),
        'sgu_w': nrm(ks[21], (N_ODD, SGU_GROUPS, SGU_CHUNK, SGU_CHUNK), SGU_CHUNK ** -0.5),
        'sgu_b': 1.0 + nrm(ks[22], (N_ODD, SGU_GROUPS, SGU_CHUNK), 0.1),
        'w_out_odd': nrm(ks[23], (N_ODD, 2 * MIX_W, D_MODEL), (2 * MIX_W) ** -0.5),
        'norm_ffn': 1.0 + nrm(ks[24], (DEPTH, D_MODEL), 0.1),
        'w_gate_up': nrm(ks[25], (DEPTH, D_MODEL, 2 * FFN_HIDDEN), D_MODEL ** -0.5),
        'w_down': nrm(ks[26], (DEPTH, FFN_HIDDEN, D_MODEL), FFN_HIDDEN ** -0.5),
    }


def reference(x_prompt, x_sample, cache_conv_a, cache_conv_b, cache_k, cache_v,
              norm_mix_even, w_in_even, conv_a_w, conv_a_b, ln_a_g, ln_a_b, conv_b_w, w_out_even,
              norm_mix_odd, w_in_odd, q_norm_g, k_norm_g, rel_bias, sgu_ln_g, sgu_ln_b, sgu_w, sgu_b,
              w_out_odd, norm_ffn, w_gate_up, w_down):
    xp, xs = x_prompt, x_sample
    conv_a_p, conv_b_p, k_p, v_p = [], [], [], []
    conv_a_s, conv_b_s, k_s, v_s, sv_s = [], [], [], [], []
    for layer in range(DEPTH):
        i = layer // 2
        if layer % 2 == 0:
            params = (w_in_even[i], conv_a_w[i], conv_a_b[i], ln_a_g[i], ln_a_b[i], conv_b_w[i], w_out_even[i])
            zero_a = jnp.zeros((xp.shape[0], CONV_A_W - 1, MIX_W), xp.dtype)
            zero_b = jnp.zeros((xp.shape[0], CONV_B_W - 1, MIX_W), xp.dtype)
            mix_p, na_p, nb_p = conv_mixers(rmsnorm(xp, norm_mix_even[i]), zero_a, zero_b, *params)
            mix_s, na_s, nb_s = conv_mixers(rmsnorm(xs, norm_mix_even[i]), cache_conv_a[i], cache_conv_b[i], *params)
            conv_a_p.append(na_p)
            conv_b_p.append(nb_p)
            conv_a_s.append(na_s)
            conv_b_s.append(nb_s)
        else:
            q, k, v, u, sv = odd_project(rmsnorm(xp, norm_mix_odd[i]), w_in_odd[i], q_norm_g[i], k_norm_g[i])
            att = band_attention_prompt(q, k, v, rel_bias[i])
            sg, _ = spatial_gating(u, sv, sgu_ln_g[i], sgu_ln_b[i], sgu_w[i], sgu_b[i])
            mix_p = jnp.concatenate([att, sg], axis=-1) @ w_out_odd[i]
            keep = min(ATT_WINDOW, xp.shape[1])
            k_p.append(k[:, -keep:])
            v_p.append(v[:, -keep:])
            q, k, v, u, sv = odd_project(rmsnorm(xs, norm_mix_odd[i]), w_in_odd[i], q_norm_g[i], k_norm_g[i])
            att = band_attention_sample(q, k, v, cache_k[i], cache_v[i], rel_bias[i])
            sg, svn = spatial_gating(u, sv, sgu_ln_g[i], sgu_ln_b[i], sgu_w[i], sgu_b[i])
            mix_s = jnp.concatenate([att, sg], axis=-1) @ w_out_odd[i]
            k_s.append(k)
            v_s.append(v)
            sv_s.append(svn)
        xp = xp + mix_p
        xs = xs + mix_s
        xp = xp + swiglu(rmsnorm(xp, norm_ffn[layer]), w_gate_up[layer], w_down[layer])
        xs = xs + swiglu(rmsnorm(xs, norm_ffn[layer]), w_gate_up[layer], w_down[layer])
    return (xp, xs,
            jnp.stack(conv_a_p), jnp.stack(conv_b_p), jnp.stack(k_p), jnp.stack(v_p),
            jnp.stack(conv_a_s), jnp.stack(conv_b_s), jnp.stack(k_s), jnp.stack(v_s), jnp.stack(sv_s))
```

```python
import functools

import jax
import jax.numpy as jnp
from jax import lax
from jax.experimental import pallas as pl
from jax.experimental.pallas import tpu as pltpu

F32 = jnp.float32
BF16 = jnp.bfloat16

D_MODEL = 1024
MIX_W = 512
FFN_HIDDEN = 2816
CONV_A_W = 31
CONV_B_W = 3
HEAD_DIM = 64
N_HEADS = 8
N_PAIRS = N_HEADS // 2
CHUNK = 64
ATT_WINDOW = 512
REL_CLIP = 256
SGU_CHUNK = 128
SGU_GROUPS = 4
NEG_INF = -1e30

LANES = 128
KEY_SPAN = ATT_WINDOW + 2 * CHUNK
A_PAD = 32
B_PAD = 8
CONV_ROWS = 32
VMEM_LIMIT = 56 * 1024 * 1024


def _resident(shape):
    zeros = (0,) * len(shape)
    return pl.BlockSpec(shape, lambda *_: zeros, pipeline_mode=pl.Buffered(1))


def _dot(a, b):
    return jnp.dot(a, b, preferred_element_type=F32)


def _rmsnorm_bf16(x, g):
    ms = jnp.mean(x * x, axis=-1, keepdims=True)
    return (x * lax.rsqrt(ms + 1e-6) * g).astype(BF16)


def _layernorm(x, g, b):
    mu = jnp.mean(x, axis=-1, keepdims=True)
    xc = x - mu
    var = jnp.mean(xc * xc, axis=-1, keepdims=True)
    return xc * lax.rsqrt(var + 1e-5) * g + b


def _even_kernel(*refs, T, has_cache):
    if has_cache:
        (x_ref, g_ref, win_ref, caw_ref, cab_ref, lng_ref, lnb_ref, cbw_ref, wout_ref, ia_ref, ib_ref,
         y_ref, na_ref, nb_ref, abuf, zbuf, gbuf, mixbuf) = refs
    else:
        (x_ref, g_ref, win_ref, caw_ref, cab_ref, lng_ref, lnb_ref, cbw_ref, wout_ref,
         y_ref, na_ref, nb_ref, abuf, zbuf, gbuf, mixbuf) = refs
    t = pl.program_id(1)

    @pl.when(t == 0)
    def _():
        if has_cache:
            abuf[0:A_PAD, :] = ia_ref[...]
            zbuf[0:B_PAD, :] = ib_ref[...]
        else:
            abuf[0:A_PAD, :] = jnp.zeros((A_PAD, MIX_W), F32)
            zbuf[0:B_PAD, :] = jnp.zeros((B_PAD, MIX_W), F32)

    x = x_ref[...]
    h = _rmsnorm_bf16(x, g_ref[...])
    a_val = _dot(h, win_ref[:, 0:MIX_W])
    a_gate = _dot(h, win_ref[:, MIX_W:2 * MIX_W])
    abuf[A_PAD:A_PAD + T, :] = a_val * jax.nn.sigmoid(a_gate)
    gbuf[...] = _dot(h, win_ref[:, 2 * MIX_W:3 * MIX_W])
    zbuf[B_PAD:B_PAD + T, :] = _dot(h, win_ref[:, 3 * MIX_W:4 * MIX_W]) * _dot(h, win_ref[:, 4 * MIX_W:5 * MIX_W])

    rows = min(T, CONV_ROWS)
    a_off = A_PAD - (CONV_A_W - 1)
    b_off = B_PAD - (CONV_B_W - 1)
    for r0 in range(0, T, rows):
        a_cols = []
        for c0 in range(0, MIX_W, LANES):
            cs = slice(c0, c0 + LANES)
            acc = caw_ref[0:1, cs] * abuf[r0 + a_off:r0 + a_off + rows, cs]
            for w in range(1, CONV_A_W):
                acc = acc + caw_ref[w:w + 1, cs] * abuf[r0 + a_off + w:r0 + a_off + w + rows, cs]
            a_cols.append(acc + cab_ref[:, cs])
            zc = cbw_ref[0:1, cs] * zbuf[r0 + b_off:r0 + b_off + rows, cs]
            for w in range(1, CONV_B_W):
                zc = zc + cbw_ref[w:w + 1, cs] * zbuf[r0 + b_off + w:r0 + b_off + w + rows, cs]
            mixbuf[r0:r0 + rows, MIX_W + c0:MIX_W + c0 + LANES] = (gbuf[r0:r0 + rows, cs] * zc).astype(BF16)
        a_conv = jnp.concatenate(a_cols, axis=-1)
        a_ln = _layernorm(a_conv, lng_ref[...], lnb_ref[...])
        mixbuf[r0:r0 + rows, 0:MIX_W] = (a_ln * jax.nn.sigmoid(a_ln)).astype(BF16)

    y_ref[...] = x + _dot(mixbuf[...], wout_ref[...])
    na_ref[...] = abuf[T + a_off:T + A_PAD, :]
    nb_ref[...] = zbuf[T + b_off:T + B_PAD, :]
    tail_a = abuf[T:T + A_PAD, :]
    tail_b = zbuf[T:T + B_PAD, :]
    abuf[0:A_PAD, :] = tail_a
    zbuf[0:B_PAD, :] = tail_b


def _even_mixer(x, g, w_in, caw, cab, lng, lnb, cbw, w_out, cache_a=None, cache_b=None, *, T):
    B, S, _ = x.shape
    has_cache = cache_a is not None
    tile = pl.BlockSpec((None, T, D_MODEL), lambda b, t: (b, t, 0))
    in_specs = [tile, _resident((1, D_MODEL)), _resident((D_MODEL, 5 * MIX_W)), _resident((CONV_A_W, MIX_W)),
                _resident((1, MIX_W)), _resident((1, MIX_W)), _resident((1, MIX_W)), _resident((CONV_B_W, MIX_W)),
                _resident((2 * MIX_W, D_MODEL))]
    args = [x, g, w_in, caw, cab, lng, lnb, cbw, w_out]
    if has_cache:
        in_specs += [pl.BlockSpec((None, A_PAD, MIX_W), lambda b, t: (b, 0, 0)),
                     pl.BlockSpec((None, B_PAD, MIX_W), lambda b, t: (b, 0, 0))]
        args += [cache_a, cache_b]
    return pl.pallas_call(
        functools.partial(_even_kernel, T=T, has_cache=has_cache),
        grid=(B, S // T),
        in_specs=in_specs,
        out_specs=[tile,
                   pl.BlockSpec((None, CONV_A_W - 1, MIX_W), lambda b, t: (b, 0, 0)),
                   pl.BlockSpec((None, CONV_B_W - 1, MIX_W), lambda b, t: (b, 0, 0))],
        out_shape=[jax.ShapeDtypeStruct((B, S, D_MODEL), F32),
                   jax.ShapeDtypeStruct((B, CONV_A_W - 1, MIX_W), F32),
                   jax.ShapeDtypeStruct((B, CONV_B_W - 1, MIX_W), F32)],
        scratch_shapes=[pltpu.VMEM((A_PAD + T, MIX_W), F32), pltpu.VMEM((B_PAD + T, MIX_W), F32),
                        pltpu.VMEM((T, MIX_W), F32), pltpu.VMEM((T, 2 * MIX_W), BF16)],
        compiler_params=pltpu.CompilerParams(dimension_semantics=("arbitrary", "arbitrary"),
                                             vmem_limit_bytes=VMEM_LIMIT),
        name="even_mixer_cache" if has_cache else "even_mixer",
    )(*args)


FFN_CHUNK = 512


def _ffn_kernel(x_ref, g_ref, wgu_ref, wd_ref, y_ref):
    x = x_ref[...]
    h = _rmsnorm_bf16(x, g_ref[...])
    acc = x
    for c0 in range(0, FFN_HIDDEN, FFN_CHUNK):
        cw = min(FFN_CHUNK, FFN_HIDDEN - c0)
        gate = _dot(h, wgu_ref[:, c0:c0 + cw])
        up = _dot(h, wgu_ref[:, FFN_HIDDEN + c0:FFN_HIDDEN + c0 + cw])
        act = (gate * jax.nn.sigmoid(gate) * up).astype(BF16)
        acc = acc + _dot(act, wd_ref[c0:c0 + cw, :])
    y_ref[...] = acc


def _ffn(x2d, g, w_gate_up, w_down, *, TM):
    M = x2d.shape[0]
    tile = pl.BlockSpec((TM, D_MODEL), lambda i: (i, 0))
    return pl.pallas_call(
        _ffn_kernel,
        grid=(M // TM,),
        in_specs=[tile, _resident((1, D_MODEL)), _resident((D_MODEL, 2 * FFN_HIDDEN)),
                  _resident((FFN_HIDDEN, D_MODEL))],
        out_specs=tile,
        out_shape=jax.ShapeDtypeStruct((M, D_MODEL), F32),
        compiler_params=pltpu.CompilerParams(dimension_semantics=("arbitrary",), vmem_limit_bytes=VMEM_LIMIT),
        name="swiglu",
    )(x2d, g, w_gate_up, w_down)


EXT_W = 768
RB_PAD = 640


def _bias_kernel(rb_ref, outp_ref, outs_ref, *, dec_seq):
    rb = rb_ref[...]
    hi = rb.astype(BF16)
    r1 = rb - hi.astype(F32)
    mid = r1.astype(BF16)
    lo = (r1 - mid.astype(F32)).astype(BF16)
    src = lax.broadcasted_iota(jnp.int32, (RB_PAD, EXT_W), 0)
    m = lax.broadcasted_iota(jnp.int32, (RB_PAD, EXT_W), 1)
    idx = jnp.clip((KEY_SPAN - 1) - m, -REL_CLIP, REL_CLIP) + REL_CLIP
    onehot = jnp.where(src == idx, 1.0, 0.0).astype(BF16)
    ext = (_dot(hi, onehot) + _dot(mid, onehot)) + _dot(lo, onehot)
    j = lax.broadcasted_iota(jnp.int32, (N_HEADS, KEY_SPAN), 1)
    for r in range(2 * CHUNK):
        row = ext[:, 2 * CHUNK - 1 - r:2 * CHUNK - 1 - r + KEY_SPAN]
        if r < CHUNK:
            valid = j < ATT_WINDOW + CHUNK
        else:
            valid = j >= CHUNK
        outp_ref[r] = jnp.where(valid, row, NEG_INF)
        if r < dec_seq:
            outs_ref[r] = jnp.where(j < ATT_WINDOW + dec_seq, row, NEG_INF)


def _bias_tables(rel_bias, dec_seq):
    rb = jnp.pad(rel_bias, ((0, 0), (0, RB_PAD - rel_bias.shape[1])))
    outp, outs = pl.pallas_call(
        functools.partial(_bias_kernel, dec_seq=dec_seq),
        out_shape=[jax.ShapeDtypeStruct((2 * CHUNK, N_HEADS, KEY_SPAN), F32),
                   jax.ShapeDtypeStruct((dec_seq, N_HEADS, KEY_SPAN), F32)],
        name="rel_bias_tables",
    )(rb)

    def pair_major(tbl):
        R = tbl.shape[0]
        return jnp.transpose(tbl, (1, 0, 2)).reshape(N_PAIRS, 2 * R, KEY_SPAN)

    return pair_major(outp), pair_major(outs)


def _odd_kernel(*refs, T, QR, LC, has_cache):
    if has_cache:
        (x_ref, g_ref, win_ref, bd_ref, qg_ref, kg_ref, bias_ref, lng_ref, lnb_ref, ws_ref, bs_ref, wout_ref,
         ck_ref, cv_ref, y_ref, ko_ref, vo_ref, svo_ref, kbuf, vbuf, qe_buf, qo_buf, mixbuf) = refs
    else:
        (x_ref, g_ref, win_ref, bd_ref, qg_ref, kg_ref, bias_ref, lng_ref, lnb_ref, ws_ref, bs_ref, wout_ref,
         y_ref, ko_ref, vo_ref, kbuf, vbuf, qe_buf, qo_buf, mixbuf) = refs
    t = pl.program_id(1)
    key_rows = kbuf.shape[0]

    @pl.when(t == 0)
    def _():
        if has_cache:
            kbuf[0:ATT_WINDOW, :] = ck_ref[...].astype(BF16)
            vbuf[0:ATT_WINDOW, :] = cv_ref[...].astype(BF16)
        else:
            kbuf[0:ATT_WINDOW, :] = jnp.zeros((ATT_WINDOW, MIX_W), BF16)
            vbuf[0:ATT_WINDOW, :] = jnp.zeros((ATT_WINDOW, MIX_W), BF16)
        if key_rows > ATT_WINDOW + T:
            pad = key_rows - ATT_WINDOW - T
            kbuf[ATT_WINDOW + T:key_rows, :] = jnp.zeros((pad, MIX_W), BF16)
            vbuf[ATT_WINDOW + T:key_rows, :] = jnp.zeros((pad, MIX_W), BF16)

    x = x_ref[...]
    h = _rmsnorm_bf16(x, g_ref[...])
    lane_half = (lax.broadcasted_iota(jnp.int32, (T, MIX_W), 1) & (LANES - 1)) < HEAD_DIM

    q = _dot(h, win_ref[:, 0:MIX_W])
    q_ms = _dot((q * q).astype(BF16), bd_ref[...]) * (1.0 / HEAD_DIM)
    qn = q * lax.rsqrt(q_ms + 1e-6) * (qg_ref[...] * HEAD_DIM ** -0.5)
    qe_buf[...] = jnp.where(lane_half, qn, 0.0).astype(BF16)
    qo_buf[...] = jnp.where(lane_half, 0.0, qn).astype(BF16)

    k = _dot(h, win_ref[:, MIX_W:2 * MIX_W])
    k_ms = _dot((k * k).astype(BF16), bd_ref[...]) * (1.0 / HEAD_DIM)
    kn = k * lax.rsqrt(k_ms + 1e-6) * kg_ref[...]
    ko_ref[...] = kn
    kbuf[ATT_WINDOW:ATT_WINDOW + T, :] = kn.astype(BF16)

    v = _dot(h, win_ref[:, 2 * MIX_W:3 * MIX_W])
    vo_ref[...] = v
    vbuf[ATT_WINDOW:ATT_WINDOW + T, :] = v.astype(BF16)

    key_idx = lax.broadcasted_iota(jnp.int32, (1, KEY_SPAN), 1)
    out_half = lax.broadcasted_iota(jnp.int32, (QR, LANES), 1) < HEAD_DIM
    for r0 in range(0, T, QR):
        if has_cache:
            pos_mask = None
        else:
            first_valid = jnp.where(t == 0, ATT_WINDOW - r0, 0)
            pos_mask = jnp.where(key_idx >= first_valid, 0.0, NEG_INF)
        for p in range(N_PAIRS):
            cs = slice(p * LANES, (p + 1) * LANES)
            lhs = jnp.concatenate([qe_buf[r0:r0 + QR, cs], qo_buf[r0:r0 + QR, cs]], axis=0)
            s = lax.dot_general(lhs, kbuf[r0:r0 + KEY_SPAN, cs], (((1,), (1,)), ((), ())),
                                preferred_element_type=F32)
            s = s + bias_ref[p]
            if pos_mask is not None:
                s = s + pos_mask
            m = jnp.max(s, axis=-1, keepdims=True)
            e = jnp.exp(s - m)
            l = jnp.sum(e, axis=-1, keepdims=True)
            o2 = _dot(e.astype(BF16), vbuf[r0:r0 + KEY_SPAN, cs]) / l
            mixbuf[r0:r0 + QR, cs] = jnp.where(out_half, o2[0:QR], o2[QR:2 * QR]).astype(BF16)

    u = _dot(h, win_ref[:, 3 * MIX_W:4 * MIX_W])
    sv = _dot(h, win_ref[:, 4 * MIX_W:5 * MIX_W])
    svn = _layernorm(sv, lng_ref[...], lnb_ref[...])
    if has_cache:
        svo_ref[...] = svn
    svb = svn.astype(BF16)
    if LC < SGU_CHUNK:
        svb = jnp.concatenate([svb, jnp.zeros((SGU_CHUNK - LC, MIX_W), BF16)], axis=0)
    tri = (lax.broadcasted_iota(jnp.int32, (SGU_CHUNK, SGU_CHUNK), 0)
           >= lax.broadcasted_iota(jnp.int32, (SGU_CHUNK, SGU_CHUNK), 1))
    for gi in range(SGU_GROUPS):
        cs = slice(gi * LANES, (gi + 1) * LANES)
        w_tril = jnp.where(tri, ws_ref[gi], 0.0).astype(BF16)[0:LC, :]
        for j0 in range(0, T, LC):
            rhs = svb[:, cs] if LC < SGU_CHUNK else svb[j0:j0 + LC, cs]
            gate = _dot(w_tril, rhs) + bs_ref[gi][0:LC, :]
            mixbuf[j0:j0 + LC, MIX_W + gi * LANES:MIX_W + (gi + 1) * LANES] = (u[j0:j0 + LC, cs] * gate).astype(BF16)

    y_ref[...] = x + _dot(mixbuf[...], wout_ref[...])
    if not has_cache:
        next_k = kbuf[T:T + ATT_WINDOW, :]
        next_v = vbuf[T:T + ATT_WINDOW, :]
        kbuf[0:ATT_WINDOW, :] = next_k
        vbuf[0:ATT_WINDOW, :] = next_v


def _odd_mixer(x, g, w_in, bd, qg, kg, bias, lng, lnb, ws, bs, w_out, cache_k=None, cache_v=None, *, T, QR, LC):
    B, S, _ = x.shape
    has_cache = cache_k is not None
    assert has_cache or T == ATT_WINDOW
    assert not has_cache or S == T
    key_rows = max(ATT_WINDOW + T, KEY_SPAN)
    keep = min(ATT_WINDOW, S)
    tile = pl.BlockSpec((None, T, D_MODEL), lambda b, t: (b, t, 0))
    kv_out = pl.BlockSpec((None, T, MIX_W), lambda b, t: (b, 0, 0))
    in_specs = [tile, _resident((1, D_MODEL)), _resident((D_MODEL, 5 * MIX_W)), _resident((MIX_W, MIX_W)),
                _resident((1, MIX_W)), _resident((1, MIX_W)), _resident(bias.shape), _resident((1, MIX_W)),
                _resident((1, MIX_W)), _resident((SGU_GROUPS, SGU_CHUNK, SGU_CHUNK)),
                _resident((SGU_GROUPS, SGU_CHUNK, LANES)), _resident((2 * MIX_W, D_MODEL))]
    args = [x, g, w_in, bd, qg, kg, bias, lng, lnb, ws, bs, w_out]
    out_specs = [tile, kv_out, kv_out]
    out_shape = [jax.ShapeDtypeStruct((B, S, D_MODEL), F32), jax.ShapeDtypeStruct((B, keep, MIX_W), F32),
                 jax.ShapeDtypeStruct((B, keep, MIX_W), F32)]
    if has_cache:
        cache_spec = pl.BlockSpec((None, ATT_WINDOW, MIX_W), lambda b, t: (b, 0, 0))
        in_specs += [cache_spec, cache_spec]
        args += [cache_k, cache_v]
        out_specs.append(kv_out)
        out_shape.append(jax.ShapeDtypeStruct((B, S, MIX_W), F32))
    return pl.pallas_call(
        functools.partial(_odd_kernel, T=T, QR=QR, LC=LC, has_cache=has_cache),
        grid=(B, S // T),
        in_specs=in_specs,
        out_specs=out_specs,
        out_shape=out_shape,
        scratch_shapes=[pltpu.VMEM((key_rows, MIX_W), BF16), pltpu.VMEM((key_rows, MIX_W), BF16),
                        pltpu.VMEM((T, MIX_W), BF16), pltpu.VMEM((T, MIX_W), BF16),
                        pltpu.VMEM((T, 2 * MIX_W), BF16)],
        compiler_params=pltpu.CompilerParams(dimension_semantics=("arbitrary", "arbitrary"),
                                             vmem_limit_bytes=VMEM_LIMIT),
        name="odd_mixer_cache" if has_cache else "odd_mixer",
    )(*args)


def kernel(x_prompt, x_sample, cache_conv_a, cache_conv_b, cache_k, cache_v, norm_mix_even, w_in_even, conv_a_w, conv_a_b, ln_a_g, ln_a_b, conv_b_w, w_out_even, norm_mix_odd, w_in_odd, q_norm_g, k_norm_g, rel_bias, sgu_ln_g, sgu_ln_b, sgu_w, sgu_b, w_out_odd, norm_ffn, w_gate_up, w_down):
    B, S, _ = x_prompt.shape
    DB, DS, _ = x_sample.shape
    assert w_gate_up.shape[0] == 2 and w_in_even.shape[0] == 1 and w_in_odd.shape[0] == 1
    assert S % ATT_WINDOW == 0 and DS <= CHUNK and DS % 16 == 0 and cache_k.shape[2] == ATT_WINDOW
    row = lambda a: a.reshape(1, -1)
    tile_rows = ATT_WINDOW

    even_w = (row(norm_mix_even[0]), w_in_even[0].astype(BF16), conv_a_w[0], row(conv_a_b[0]), row(ln_a_g[0]),
              row(ln_a_b[0]), conv_b_w[0], w_out_even[0].astype(BF16))
    ffn0_w = (row(norm_ffn[0]), w_gate_up[0].astype(BF16), w_down[0].astype(BF16))
    xp, conv_a_p, conv_b_p = _even_mixer(x_prompt, *even_w, T=tile_rows)
    ca = jnp.pad(cache_conv_a[0], ((0, 0), (A_PAD - (CONV_A_W - 1), 0), (0, 0)))
    cb = jnp.pad(cache_conv_b[0], ((0, 0), (B_PAD - (CONV_B_W - 1), 0), (0, 0)))
    xs, conv_a_s, conv_b_s = _even_mixer(x_sample, *even_w, ca, cb, T=DS)
    xp = _ffn(xp.reshape(B * S, D_MODEL), *ffn0_w, TM=tile_rows).reshape(B, S, D_MODEL)
    xs = _ffn(xs.reshape(DB * DS, D_MODEL), *ffn0_w, TM=DB * DS).reshape(DB, DS, D_MODEL)

    bias_p, bias_s = _bias_tables(rel_bias[0], DS)
    head_ones = jnp.kron(jnp.eye(N_HEADS, dtype=F32), jnp.ones((HEAD_DIM, HEAD_DIM), F32)).astype(BF16)
    sgu_bias = jnp.broadcast_to(sgu_b[0][:, :, None], (SGU_GROUPS, SGU_CHUNK, LANES))
    odd_w = (row(norm_mix_odd[0]), w_in_odd[0].astype(BF16), head_ones, row(jnp.tile(q_norm_g[0], N_HEADS)),
             row(jnp.tile(k_norm_g[0], N_HEADS)))
    odd_w2 = (row(sgu_ln_g[0]), row(sgu_ln_b[0]), sgu_w[0], sgu_bias, w_out_odd[0].astype(BF16))
    ffn1_w = (row(norm_ffn[1]), w_gate_up[1].astype(BF16), w_down[1].astype(BF16))
    xp, k_p, v_p = _odd_mixer(xp, *odd_w, bias_p, *odd_w2, T=tile_rows, QR=2 * CHUNK, LC=SGU_CHUNK)
    ck = cache_k[0].reshape(DB, ATT_WINDOW, MIX_W)
    cv = cache_v[0].reshape(DB, ATT_WINDOW, MIX_W)
    xs, k_s, v_s, sv_s = _odd_mixer(xs, *odd_w, bias_s, *odd_w2, ck, cv, T=DS, QR=DS, LC=min(DS, SGU_CHUNK))
    xp = _ffn(xp.reshape(B * S, D_MODEL), *ffn1_w, TM=tile_rows).reshape(B, S, D_MODEL)
    xs = _ffn(xs.reshape(DB * DS, D_MODEL), *ffn1_w, TM=DB * DS).reshape(DB, DS, D_MODEL)

    heads = lambda a: a.reshape(1, a.shape[0], a.shape[1], N_HEADS, HEAD_DIM)
    return (xp, xs, conv_a_p[None], conv_b_p[None], heads(k_p), heads(v_p),
            conv_a_s[None], conv_b_s[None], heads(k_s), heads(v_s), sv_s[None])
```

```python
import functools

import jax
import jax.numpy as jnp
from jax import lax
from jax.experimental import pallas as pl
from jax.experimental.pallas import tpu as pltpu

F32 = jnp.float32
BF16 = jnp.bfloat16

D_MODEL = 1024
MIX_W = 512
FFN_HIDDEN = 2816
CONV_A_W = 31
CONV_B_W = 3
HEAD_DIM = 64
N_HEADS = 8
N_PAIRS = N_HEADS // 2
CHUNK = 64
ATT_WINDOW = 512
REL_CLIP = 256
SGU_CHUNK = 128
SGU_GROUPS = 4
NEG_INF = -1e30

LANES = 128
SUBLANES = 8
KEY_SPAN = ATT_WINDOW + 2 * CHUNK
A_PAD = 32
B_PAD = 8
CONV_ROWS = 32
EVEN_SUB_ROWS = 256
VMEM_LIMIT = 56 * 1024 * 1024


def _resident(shape):
    zeros = (0,) * len(shape)
    return pl.BlockSpec(shape, lambda *_: zeros, pipeline_mode=pl.Buffered(1))


def _dot(a, b):
    return jnp.dot(a, b, preferred_element_type=F32)


def _rmsnorm_bf16(x, g):
    ms = jnp.mean(x * x, axis=-1, keepdims=True)
    return (x * lax.rsqrt(ms + 1e-6) * g).astype(BF16)


def _layernorm(x, g, b):
    mu = jnp.mean(x, axis=-1, keepdims=True)
    xc = x - mu
    var = jnp.mean(xc * xc, axis=-1, keepdims=True)
    return xc * lax.rsqrt(var + 1e-5) * g + b


def _even_kernel(*refs, T, SUB, has_cache):
    if has_cache:
        (x_ref, g_ref, win_ref, caw_ref, cab_ref, lng_ref, lnb_ref, cbw_ref, wout_ref, ia_ref, ib_ref,
         y_ref, na_ref, nb_ref, abuf, zbuf, gbuf, mixbuf, ashift, zshift) = refs
    else:
        (x_ref, g_ref, win_ref, caw_ref, cab_ref, lng_ref, lnb_ref, cbw_ref, wout_ref,
         y_ref, na_ref, nb_ref, abuf, zbuf, gbuf, mixbuf, ashift, zshift) = refs
    t = pl.program_id(1)

    @pl.when(t == 0)
    def _():
        if has_cache:
            abuf[0:A_PAD, :] = ia_ref[...]
            zbuf[0:B_PAD, :] = ib_ref[...]
        else:
            abuf[0:A_PAD, :] = jnp.zeros((A_PAD, MIX_W), F32)
            zbuf[0:B_PAD, :] = jnp.zeros((B_PAD, MIX_W), F32)

    subs = range(0, T, SUB)
    for h0 in subs:
        h = _rmsnorm_bf16(x_ref[h0:h0 + SUB, :], g_ref[...])
        a_val = _dot(h, win_ref[:, 0:MIX_W])
        a_gate = _dot(h, win_ref[:, MIX_W:2 * MIX_W])
        abuf[A_PAD + h0:A_PAD + h0 + SUB, :] = a_val * jax.nn.sigmoid(a_gate)
        gbuf[h0:h0 + SUB, :] = _dot(h, win_ref[:, 2 * MIX_W:3 * MIX_W])
        zbuf[B_PAD + h0:B_PAD + h0 + SUB, :] = (_dot(h, win_ref[:, 3 * MIX_W:4 * MIX_W])
                                                 * _dot(h, win_ref[:, 4 * MIX_W:5 * MIX_W]))

    rows = min(SUB, CONV_ROWS)
    for si, h0 in enumerate(subs):
        for c0 in range(0, MIX_W, LANES):
            cs = slice(c0, c0 + LANES)
            a_full = abuf[h0:h0 + SUB + A_PAD, cs]
            for r in range(1, SUBLANES):
                ashift[si, r - 1, :, cs] = pltpu.roll(a_full, r, axis=0)
            z_full = zbuf[h0:h0 + SUB + B_PAD, cs]
            for r in range(1, CONV_B_W):
                zshift[si, r - 1, :, cs] = pltpu.roll(z_full, r, axis=0)
        for l0 in range(0, SUB, rows):
            a_cols = []
            for c0 in range(0, MIX_W, LANES):
                cs = slice(c0, c0 + LANES)
                acc = None
                for w in range(CONV_A_W):
                    q, r = divmod(CONV_A_W - 1 - w, SUBLANES)
                    j0 = A_PAD + l0 - SUBLANES * q
                    src = abuf[h0 + j0:h0 + j0 + rows, cs] if r == 0 else ashift[si, r - 1, j0:j0 + rows, cs]
                    term = caw_ref[w:w + 1, cs] * src
                    acc = term if acc is None else acc + term
                a_cols.append(acc + cab_ref[:, cs])
                zc = None
                for w in range(CONV_B_W):
                    r = CONV_B_W - 1 - w
                    j0 = B_PAD + l0
                    src = zbuf[h0 + j0:h0 + j0 + rows, cs] if r == 0 else zshift[si, r - 1, j0:j0 + rows, cs]
                    term = cbw_ref[w:w + 1, cs] * src
                    zc = term if zc is None else zc + term
                mixbuf[h0 + l0:h0 + l0 + rows, MIX_W + c0:MIX_W + c0 + LANES] = (
                    gbuf[h0 + l0:h0 + l0 + rows, cs] * zc).astype(BF16)
            a_ln = _layernorm(jnp.concatenate(a_cols, axis=-1), lng_ref[...], lnb_ref[...])
            mixbuf[h0 + l0:h0 + l0 + rows, 0:MIX_W] = (a_ln * jax.nn.sigmoid(a_ln)).astype(BF16)

    for h0 in subs:
        y_ref[h0:h0 + SUB, :] = x_ref[h0:h0 + SUB, :] + _dot(mixbuf[h0:h0 + SUB, :], wout_ref[...])
    na_ref[...] = abuf[T + A_PAD - (CONV_A_W - 1):T + A_PAD, :]
    nb_ref[...] = zbuf[T + B_PAD - (CONV_B_W - 1):T + B_PAD, :]
    tail_a = abuf[T:T + A_PAD, :]
    tail_b = zbuf[T:T + B_PAD, :]
    abuf[0:A_PAD, :] = tail_a
    zbuf[0:B_PAD, :] = tail_b


def _even_mixer(x, g, w_in, caw, cab, lng, lnb, cbw, w_out, cache_a=None, cache_b=None, *, T):
    B, S, _ = x.shape
    has_cache = cache_a is not None
    sub = min(T, EVEN_SUB_ROWS)
    tile = pl.BlockSpec((None, T, D_MODEL), lambda b, t: (b, t, 0))
    in_specs = [tile, _resident((1, D_MODEL)), _resident((D_MODEL, 5 * MIX_W)), _resident((CONV_A_W, MIX_W)),
                _resident((1, MIX_W)), _resident((1, MIX_W)), _resident((1, MIX_W)), _resident((CONV_B_W, MIX_W)),
                _resident((2 * MIX_W, D_MODEL))]
    args = [x, g, w_in, caw, cab, lng, lnb, cbw, w_out]
    if has_cache:
        in_specs += [pl.BlockSpec((None, A_PAD, MIX_W), lambda b, t: (b, 0, 0)),
                     pl.BlockSpec((None, B_PAD, MIX_W), lambda b, t: (b, 0, 0))]
        args += [cache_a, cache_b]
    return pl.pallas_call(
        functools.partial(_even_kernel, T=T, SUB=sub, has_cache=has_cache),
        grid=(B, S // T),
        in_specs=in_specs,
        out_specs=[tile,
                   pl.BlockSpec((None, CONV_A_W - 1, MIX_W), lambda b, t: (b, 0, 0)),
                   pl.BlockSpec((None, CONV_B_W - 1, MIX_W), lambda b, t: (b, 0, 0))],
        out_shape=[jax.ShapeDtypeStruct((B, S, D_MODEL), F32),
                   jax.ShapeDtypeStruct((B, CONV_A_W - 1, MIX_W), F32),
                   jax.ShapeDtypeStruct((B, CONV_B_W - 1, MIX_W), F32)],
        scratch_shapes=[pltpu.VMEM((A_PAD + T, MIX_W), F32), pltpu.VMEM((B_PAD + T, MIX_W), F32),
                        pltpu.VMEM((T, MIX_W), F32), pltpu.VMEM((T, 2 * MIX_W), BF16),
                        pltpu.VMEM((T // sub, SUBLANES - 1, sub + A_PAD, MIX_W), F32),
                        pltpu.VMEM((T // sub, CONV_B_W - 1, sub + B_PAD, MIX_W), F32)],
        compiler_params=pltpu.CompilerParams(dimension_semantics=("arbitrary", "arbitrary"),
                                             vmem_limit_bytes=VMEM_LIMIT),
        name="even_mixer_cache" if has_cache else "even_mixer",
    )(*args)


FFN_CHUNK = 512


def _ffn_kernel(x_ref, g_ref, wgu_ref, wd_ref, y_ref):
    x = x_ref[...]
    h = _rmsnorm_bf16(x, g_ref[...])
    acc = x
    for c0 in range(0, FFN_HIDDEN, FFN_CHUNK):
        cw = min(FFN_CHUNK, FFN_HIDDEN - c0)
        gate = _dot(h, wgu_ref[:, c0:c0 + cw])
        up = _dot(h, wgu_ref[:, FFN_HIDDEN + c0:FFN_HIDDEN + c0 + cw])
        act = (gate * jax.nn.sigmoid(gate) * up).astype(BF16)
        acc = acc + _dot(act, wd_ref[c0:c0 + cw, :])
    y_ref[...] = acc


def _ffn(x2d, g, w_gate_up, w_down, *, TM):
    M = x2d.shape[0]
    tile = pl.BlockSpec((TM, D_MODEL), lambda i: (i, 0))
    return pl.pallas_call(
        _ffn_kernel,
        grid=(M // TM,),
        in_specs=[tile, _resident((1, D_MODEL)), _resident((D_MODEL, 2 * FFN_HIDDEN)),
                  _resident((FFN_HIDDEN, D_MODEL))],
        out_specs=tile,
        out_shape=jax.ShapeDtypeStruct((M, D_MODEL), F32),
        compiler_params=pltpu.CompilerParams(dimension_semantics=("arbitrary",), vmem_limit_bytes=VMEM_LIMIT),
        name="swiglu",
    )(x2d, g, w_gate_up, w_down)


EXT_W = 768
RB_PAD = 640


def _bias_kernel(rb_ref, outp_ref, outs_ref, *, dec_seq):
    rb = rb_ref[...]
    hi = rb.astype(BF16)
    r1 = rb - hi.astype(F32)
    mid = r1.astype(BF16)
    lo = (r1 - mid.astype(F32)).astype(BF16)
    src = lax.broadcasted_iota(jnp.int32, (RB_PAD, EXT_W), 0)
    m = lax.broadcasted_iota(jnp.int32, (RB_PAD, EXT_W), 1)
    idx = jnp.clip((KEY_SPAN - 1) - m, -REL_CLIP, REL_CLIP) + REL_CLIP
    onehot = jnp.where(src == idx, 1.0, 0.0).astype(BF16)
    ext = (_dot(hi, onehot) + _dot(mid, onehot)) + _dot(lo, onehot)
    j = lax.broadcasted_iota(jnp.int32, (N_HEADS, KEY_SPAN), 1)
    for r in range(2 * CHUNK):
        row = ext[:, 2 * CHUNK - 1 - r:2 * CHUNK - 1 - r + KEY_SPAN]
        if r < CHUNK:
            valid = j < ATT_WINDOW + CHUNK
        else:
            valid = j >= CHUNK
        outp_ref[r] = jnp.where(valid, row, NEG_INF)
        if r < dec_seq:
            outs_ref[r] = jnp.where(j < ATT_WINDOW + dec_seq, row, NEG_INF)


def _bias_tables(rel_bias, dec_seq):
    rb = jnp.pad(rel_bias, ((0, 0), (0, RB_PAD - rel_bias.shape[1])))
    outp, outs = pl.pallas_call(
        functools.partial(_bias_kernel, dec_seq=dec_seq),
        out_shape=[jax.ShapeDtypeStruct((2 * CHUNK, N_HEADS, KEY_SPAN), F32),
                   jax.ShapeDtypeStruct((dec_seq, N_HEADS, KEY_SPAN), F32)],
        name="rel_bias_tables",
    )(rb)

    def pair_major(tbl):
        R = tbl.shape[0]
        return jnp.transpose(tbl, (1, 0, 2)).reshape(N_PAIRS, 2 * R, KEY_SPAN)

    return pair_major(outp), pair_major(outs)


def _odd_kernel(*refs, T, QR, LC, has_cache):
    if has_cache:
        (x_ref, g_ref, win_ref, bd_ref, qg_ref, kg_ref, bias_ref, lng_ref, lnb_ref, ws_ref, bs_ref, wout_ref,
         ck_ref, cv_ref, y_ref, ko_ref, vo_ref, svo_ref, kbuf, vbuf, qe_buf, qo_buf, mixbuf) = refs
    else:
        (x_ref, g_ref, win_ref, bd_ref, qg_ref, kg_ref, bias_ref, lng_ref, lnb_ref, ws_ref, bs_ref, wout_ref,
         y_ref, ko_ref, vo_ref, kbuf, vbuf, qe_buf, qo_buf, mixbuf) = refs
    t = pl.program_id(1)
    key_rows = kbuf.shape[0]

    @pl.when(t == 0)
    def _():
        if has_cache:
            kbuf[0:ATT_WINDOW, :] = ck_ref[...].astype(BF16)
            vbuf[0:ATT_WINDOW, :] = cv_ref[...].astype(BF16)
        else:
            kbuf[0:ATT_WINDOW, :] = jnp.zeros((ATT_WINDOW, MIX_W), BF16)
            vbuf[0:ATT_WINDOW, :] = jnp.zeros((ATT_WINDOW, MIX_W), BF16)
        if key_rows > ATT_WINDOW + T:
            pad = key_rows - ATT_WINDOW - T
            kbuf[ATT_WINDOW + T:key_rows, :] = jnp.zeros((pad, MIX_W), BF16)
            vbuf[ATT_WINDOW + T:key_rows, :] = jnp.zeros((pad, MIX_W), BF16)

    x = x_ref[...]
    h = _rmsnorm_bf16(x, g_ref[...])
    lane_half = (lax.broadcasted_iota(jnp.int32, (T, MIX_W), 1) & (LANES - 1)) < HEAD_DIM

    q = _dot(h, win_ref[:, 0:MIX_W])
    q_ms = _dot((q * q).astype(BF16), bd_ref[...]) * (1.0 / HEAD_DIM)
    qn = q * lax.rsqrt(q_ms + 1e-6) * (qg_ref[...] * HEAD_DIM ** -0.5)
    qe_buf[...] = jnp.where(lane_half, qn, 0.0).astype(BF16)
    qo_buf[...] = jnp.where(lane_half, 0.0, qn).astype(BF16)

    k = _dot(h, win_ref[:, MIX_W:2 * MIX_W])
    k_ms = _dot((k * k).astype(BF16), bd_ref[...]) * (1.0 / HEAD_DIM)
    kn = k * lax.rsqrt(k_ms + 1e-6) * kg_ref[...]
    ko_ref[...] = kn
    kbuf[ATT_WINDOW:ATT_WINDOW + T, :] = kn.astype(BF16)

    v = _dot(h, win_ref[:, 2 * MIX_W:3 * MIX_W])
    vo_ref[...] = v
    vbuf[ATT_WINDOW:ATT_WINDOW + T, :] = v.astype(BF16)

    key_idx = lax.broadcasted_iota(jnp.int32, (1, KEY_SPAN), 1)
    out_half = lax.broadcasted_iota(jnp.int32, (QR, LANES), 1) < HEAD_DIM
    for r0 in range(0, T, QR):
        if has_cache:
            pos_mask = None
        else:
            first_valid = jnp.where(t == 0, ATT_WINDOW - r0, 0)
            pos_mask = jnp.where(key_idx >= first_valid, 0.0, NEG_INF)
        for p in range(N_PAIRS):
            cs = slice(p * LANES, (p + 1) * LANES)
            lhs = jnp.concatenate([qe_buf[r0:r0 + QR, cs], qo_buf[r0:r0 + QR, cs]], axis=0)
            s = lax.dot_general(lhs, kbuf[r0:r0 + KEY_SPAN, cs], (((1,), (1,)), ((), ())),
                                preferred_element_type=F32)
            s = s + bias_ref[p]
            if pos_mask is not None:
                s = s + pos_mask
            m = jnp.max(s, axis=-1, keepdims=True)
            e = jnp.exp(s - m)
            l = jnp.sum(e, axis=-1, keepdims=True)
            o2 = _dot(e.astype(BF16), vbuf[r0:r0 + KEY_SPAN, cs]) / l
            mixbuf[r0:r0 + QR, cs] = jnp.where(out_half, o2[0:QR], o2[QR:2 * QR]).astype(BF16)

    u = _dot(h, win_ref[:, 3 * MIX_W:4 * MIX_W])
    sv = _dot(h, win_ref[:, 4 * MIX_W:5 * MIX_W])
    svn = _layernorm(sv, lng_ref[...], lnb_ref[...])
    if has_cache:
        svo_ref[...] = svn
    svb = svn.astype(BF16)
    if LC < SGU_CHUNK:
        svb = jnp.concatenate([svb, jnp.zeros((SGU_CHUNK - LC, MIX_W), BF16)], axis=0)
    tri = (lax.broadcasted_iota(jnp.int32, (SGU_CHUNK, SGU_CHUNK), 0)
           >= lax.broadcasted_iota(jnp.int32, (SGU_CHUNK, SGU_CHUNK), 1))
    for gi in range(SGU_GROUPS):
        cs = slice(gi * LANES, (gi + 1) * LANES)
        w_tril = jnp.where(tri, ws_ref[gi], 0.0).astype(BF16)[0:LC, :]
        for j0 in range(0, T, LC):
            rhs = svb[:, cs] if LC < SGU_CHUNK else svb[j0:j0 + LC, cs]
            gate = _dot(w_tril, rhs) + bs_ref[gi][0:LC, :]
            mixbuf[j0:j0 + LC, MIX_W + gi * LANES:MIX_W + (gi + 1) * LANES] = (u[j0:j0 + LC, cs] * gate).astype(BF16)

    y_ref[...] = x + _dot(mixbuf[...], wout_ref[...])
    if not has_cache:
        next_k = kbuf[T:T + ATT_WINDOW, :]
        next_v = vbuf[T:T + ATT_WINDOW, :]
        kbuf[0:ATT_WINDOW, :] = next_k
        vbuf[0:ATT_WINDOW, :] = next_v


def _odd_mixer(x, g, w_in, bd, qg, kg, bias, lng, lnb, ws, bs, w_out, cache_k=None, cache_v=None, *, T, QR, LC):
    B, S, _ = x.shape
    has_cache = cache_k is not None
    assert has_cache or T == ATT_WINDOW
    assert not has_cache or S == T
    key_rows = max(ATT_WINDOW + T, KEY_SPAN)
    keep = min(ATT_WINDOW, S)
    tile = pl.BlockSpec((None, T, D_MODEL), lambda b, t: (b, t, 0))
    kv_out = pl.BlockSpec((None, T, MIX_W), lambda b, t: (b, 0, 0))
    in_specs = [tile, _resident((1, D_MODEL)), _resident((D_MODEL, 5 * MIX_W)), _resident((MIX_W, MIX_W)),
                _resident((1, MIX_W)), _resident((1, MIX_W)), _resident(bias.shape), _resident((1, MIX_W)),
                _resident((1, MIX_W)), _resident((SGU_GROUPS, SGU_CHUNK, SGU_CHUNK)),
                _resident((SGU_GROUPS, SGU_CHUNK, LANES)), _resident((2 * MIX_W, D_MODEL))]
    args = [x, g, w_in, bd, qg, kg, bias, lng, lnb, ws, bs, w_out]
    out_specs = [tile, kv_out, kv_out]
    out_shape = [jax.ShapeDtypeStruct((B, S, D_MODEL), F32), jax.ShapeDtypeStruct((B, keep, MIX_W), F32),
                 jax.ShapeDtypeStruct((B, keep, MIX_W), F32)]
    if has_cache:
        cache_spec = pl.BlockSpec((None, ATT_WINDOW, MIX_W), lambda b, t: (b, 0, 0))
        in_specs += [cache_spec, cache_spec]
        args += [cache_k, cache_v]
        out_specs.append(kv_out)
        out_shape.append(jax.ShapeDtypeStruct((B, S, MIX_W), F32))
    return pl.pallas_call(
        functools.partial(_odd_kernel, T=T, QR=QR, LC=LC, has_cache=has_cache),
        grid=(B, S // T),
        in_specs=in_specs,
        out_specs=out_specs,
        out_shape=out_shape,
        scratch_shapes=[pltpu.VMEM((key_rows, MIX_W), BF16), pltpu.VMEM((key_rows, MIX_W), BF16),
                        pltpu.VMEM((T, MIX_W), BF16), pltpu.VMEM((T, MIX_W), BF16),
                        pltpu.VMEM((T, 2 * MIX_W), BF16)],
        compiler_params=pltpu.CompilerParams(dimension_semantics=("arbitrary", "arbitrary"),
                                             vmem_limit_bytes=VMEM_LIMIT),
        name="odd_mixer_cache" if has_cache else "odd_mixer",
    )(*args)


def kernel(x_prompt, x_sample, cache_conv_a, cache_conv_b, cache_k, cache_v, norm_mix_even, w_in_even, conv_a_w, conv_a_b, ln_a_g, ln_a_b, conv_b_w, w_out_even, norm_mix_odd, w_in_odd, q_norm_g, k_norm_g, rel_bias, sgu_ln_g, sgu_ln_b, sgu_w, sgu_b, w_out_odd, norm_ffn, w_gate_up, w_down):
    B, S, _ = x_prompt.shape
    DB, DS, _ = x_sample.shape
    assert w_gate_up.shape[0] == 2 and w_in_even.shape[0] == 1 and w_in_odd.shape[0] == 1
    assert S % ATT_WINDOW == 0 and DS <= CHUNK and DS % 16 == 0 and cache_k.shape[2] == ATT_WINDOW
    row = lambda a: a.reshape(1, -1)
    tile_rows = ATT_WINDOW

    even_w = (row(norm_mix_even[0]), w_in_even[0].astype(BF16), conv_a_w[0], row(conv_a_b[0]), row(ln_a_g[0]),
              row(ln_a_b[0]), conv_b_w[0], w_out_even[0].astype(BF16))
    ffn0_w = (row(norm_ffn[0]), w_gate_up[0].astype(BF16), w_down[0].astype(BF16))
    xp, conv_a_p, conv_b_p = _even_mixer(x_prompt, *even_w, T=tile_rows)
    ca = jnp.pad(cache_conv_a[0], ((0, 0), (A_PAD - (CONV_A_W - 1), 0), (0, 0)))
    cb = jnp.pad(cache_conv_b[0], ((0, 0), (B_PAD - (CONV_B_W - 1), 0), (0, 0)))
    xs, conv_a_s, conv_b_s = _even_mixer(x_sample, *even_w, ca, cb, T=DS)
    xp = _ffn(xp.reshape(B * S, D_MODEL), *ffn0_w, TM=tile_rows).reshape(B, S, D_MODEL)
    xs = _ffn(xs.reshape(DB * DS, D_MODEL), *ffn0_w, TM=DB * DS).reshape(DB, DS, D_MODEL)

    bias_p, bias_s = _bias_tables(rel_bias[0], DS)
    head_ones = jnp.kron(jnp.eye(N_HEADS, dtype=F32), jnp.ones((HEAD_DIM, HEAD_DIM), F32)).astype(BF16)
    sgu_bias = jnp.broadcast_to(sgu_b[0][:, :, None], (SGU_GROUPS, SGU_CHUNK, LANES))
    odd_w = (row(norm_mix_odd[0]), w_in_odd[0].astype(BF16), head_ones, row(jnp.tile(q_norm_g[0], N_HEADS)),
             row(jnp.tile(k_norm_g[0], N_HEADS)))
    odd_w2 = (row(sgu_ln_g[0]), row(sgu_ln_b[0]), sgu_w[0], sgu_bias, w_out_odd[0].astype(BF16))
    ffn1_w = (row(norm_ffn[1]), w_gate_up[1].astype(BF16), w_down[1].astype(BF16))
    xp, k_p, v_p = _odd_mixer(xp, *odd_w, bias_p, *odd_w2, T=tile_rows, QR=2 * CHUNK, LC=SGU_CHUNK)
    ck = cache_k[0].reshape(DB, ATT_WINDOW, MIX_W)
    cv = cache_v[0].reshape(DB, ATT_WINDOW, MIX_W)
    xs, k_s, v_s, sv_s = _odd_mixer(xs, *odd_w, bias_s, *odd_w2, ck, cv, T=DS, QR=DS, LC=min(DS, SGU_CHUNK))
    xp = _ffn(xp.reshape(B * S, D_MODEL), *ffn1_w, TM=tile_rows).reshape(B, S, D_MODEL)
    xs = _ffn(xs.reshape(DB * DS, D_MODEL), *ffn1_w, TM=DB * DS).reshape(DB, DS, D_MODEL)

    heads = lambda a: a.reshape(1, a.shape[0], a.shape[1], N_HEADS, HEAD_DIM)
    return (xp, xs, conv_a_p[None], conv_b_p[None], heads(k_p), heads(v_p),
            conv_a_s[None], conv_b_s[None], heads(k_s), heads(v_s), sv_s[None])
```
